```python
import math, functools
import jax, jax.numpy as jnp
from jax import lax
import numpy as np


D_MODEL = 1024
BATCH = 4
SEQ = 8192
DEPTH = 1
DEC_BATCH = 128
DEC_SEQ = 4
PAST_LEN = 8192
PAGE_SIZE = 128

N_HEADS = 8
HEAD_DIM = 64
V_DIM = 2 * HEAD_DIM
D_CONV = 512
CONV_WIDTH = 3
D_FF = 2816
N_SUB = 3
N_ADA = 3 * N_SUB
MACARON_W = 0.5
Q_BLOCK = 128
RMS_EPS = 1e-6
QK_COLS = N_HEADS * 2 * HEAD_DIM
V_COLS = N_HEADS * V_DIM
N_IN = 3 * D_CONV + 2 * QK_COLS + V_COLS + 2 * D_MODEL

kernel_name = 'hybrid_conv_diffattn_macaron_decode_step'


def _rmsnorm(x, g):
    xf = x.astype(jnp.float32)
    y = xf * lax.rsqrt(jnp.mean(xf * xf, axis=-1, keepdims=True) + RMS_EPS)
    return (y * g.astype(jnp.float32)).astype(x.dtype)


def _swiglu(h, w_gu, w_down):
    gate, up = jnp.split(h @ w_gu, 2, axis=-1)
    return (jax.nn.silu(gate) * up) @ w_down


def _causal_short_conv(u, prev, w, b):
    t = u.shape[1]
    up = jnp.concatenate([prev, u], axis=1)
    out = b + w[0] * up[:, 0:t]
    for j in range(1, CONV_WIDTH):
        out = out + w[j] * up[:, j:j + t]
    return out, up[:, up.shape[1] - (CONV_WIDTH - 1):]


def _diff_attention(q, k, v, q_pos, k_pos, lam):
    s = jnp.einsum('bqhcd,bkhcd->bhcqk', q, k).astype(jnp.float32) * (HEAD_DIM ** -0.5)
    causal = k_pos[None, :] <= q_pos[:, None]
    s = jnp.where(causal, s, -jnp.inf)
    p = jax.nn.softmax(s, axis=-1)
    a = p[:, :, 0] - lam * p[:, :, 1]
    return jnp.einsum('bhqk,bkhe->bqhe', a.astype(v.dtype), v)


def _attend_prompt(q, k, v, lam):
    seq = q.shape[1]
    k_pos = jnp.arange(seq)

    def block(i):
        start = i * Q_BLOCK
        qb = lax.dynamic_slice_in_dim(q, start, Q_BLOCK, axis=1)
        return _diff_attention(qb, k, v, start + jnp.arange(Q_BLOCK), k_pos, lam)

    o = lax.map(block, jnp.arange(seq // Q_BLOCK))
    return jnp.moveaxis(o, 0, 1).reshape(q.shape[0], seq, N_HEADS, V_DIM)


def _attend_sample(q, k, v, cache_k, cache_v, page_table, layer, lam):
    past = page_table.shape[1] * cache_k.shape[2]
    dec = q.shape[1]
    q_pos = past + jnp.arange(dec)
    k_pos = jnp.arange(past + dec)

    def one(args):
        pt, qs, ks, vs = args
        kp = cache_k[layer, pt].reshape((past,) + cache_k.shape[3:])
        vp = cache_v[layer, pt].reshape((past,) + cache_v.shape[3:])
        kk = jnp.concatenate([kp, ks.astype(kp.dtype)], axis=0)
        vv = jnp.concatenate([vp, vs.astype(vp.dtype)], axis=0)
        return _diff_attention(qs[None], kk[None], vv[None], q_pos, k_pos, lam)[0]

    return lax.map(one, (page_table, q, k, v))


def _mixer(h, conv_prev, attend, p, lam_init):
    b, t, _ = h.shape
    sizes = [D_CONV, D_CONV, D_CONV, QK_COLS, QK_COLS, V_COLS, D_MODEL, D_MODEL]
    cuts = np.cumsum(sizes)[:-1].tolist()
    xc, bg, cg, q, k, v, g_conv, g_attn = jnp.split(h @ p['w_in'], cuts, axis=-1)
    conv_out, conv_tail = _causal_short_conv(cg * xc, conv_prev, p['conv_w'], p['conv_b'])
    y_conv = bg * conv_out
    q = q.reshape(b, t, N_HEADS, 2, HEAD_DIM)
    k = k.reshape(b, t, N_HEADS, 2, HEAD_DIM)
    v = v.reshape(b, t, N_HEADS, V_DIM)
    o = attend(q, k, v)
    y_attn = (_rmsnorm(o, p['subln_w']) * (1.0 - lam_init)).reshape(b, t, V_COLS)
    merged = (jax.nn.sigmoid(g_conv) * (y_conv @ p['w_br_conv'])
              + jax.nn.sigmoid(g_attn) * (y_attn @ p['w_br_attn']))
    return merged @ p['w_out'], conv_tail, k, v


def _layer(x, c, conv_prev, attend, p, lam_init):
    mod = (c @ p['w_ada'] + p['b_ada']).reshape(c.shape[0], 1, N_ADA, D_MODEL)

    def modulated(x, i):
        return _rmsnorm(x, p['norm_pre'][i]) * (1.0 + mod[:, :, 3 * i + 1]) + mod[:, :, 3 * i]

    def residual(x, i, out, res_w):
        return x + res_w * mod[:, :, 3 * i + 2] * _rmsnorm(out, p['norm_post'][i])

    x = residual(x, 0, _swiglu(modulated(x, 0), p['ffn1_w_gu'], p['ffn1_w_down']), MACARON_W)
    mix, conv_tail, k, v = _mixer(modulated(x, 1), conv_prev, attend, p, lam_init)
    x = residual(x, 1, mix, 1.0)
    x = residual(x, 2, _swiglu(modulated(x, 2), p['ffn2_w_gu'], p['ffn2_w_down']), MACARON_W)
    return x, conv_tail, k, v


def setup_inputs(seed: int = 0) -> dict:
    key = jax.random.key(seed)
    ks = jax.random.split(key, 32)
    n_pages = PAST_LEN // PAGE_SIZE
    n_pool = (DEC_BATCH * n_pages * 5) // 4
    L = DEPTH

    def nrm(k, shape, scale=1.0):
        return scale * jax.random.normal(k, shape, jnp.float32)

    page_table = jax.random.permutation(ks[7], n_pool)[:DEC_BATCH * n_pages]
    page_table = page_table.reshape(DEC_BATCH, n_pages).astype(jnp.int32)
    return {
        'x_prompt': nrm(ks[0], (BATCH, SEQ, D_MODEL)),
        'x_sample': nrm(ks[1], (DEC_BATCH, DEC_SEQ, D_MODEL)),
        'c_prompt': nrm(ks[2], (BATCH, D_MODEL)),
        'c_sample': nrm(ks[3], (DEC_BATCH, D_MODEL)),
        'cache_k': nrm(ks[4], (L, n_pool, PAGE_SIZE, N_HEADS, 2, HEAD_DIM)),
        'cache_v': nrm(ks[5], (L, n_pool, PAGE_SIZE, N_HEADS, V_DIM)),
        'state_conv': nrm(ks[6], (L, DEC_BATCH, CONV_WIDTH - 1, D_CONV)),
        'page_table': page_table,
        'w_ada': nrm(ks[8], (L, D_MODEL, N_ADA * D_MODEL), 0.5 * D_MODEL ** -0.5),
        'b_ada': nrm(ks[9], (L, N_ADA * D_MODEL), 0.02),
        'norm_pre': 1.0 + nrm(ks[10], (L, N_SUB, D_MODEL), 0.05),
        'norm_post': 1.0 + nrm(ks[11], (L, N_SUB, D_MODEL), 0.05),
        'ffn1_w_gu': nrm(ks[12], (L, D_MODEL, 2 * D_FF), D_MODEL ** -0.5),
        'ffn1_w_down': nrm(ks[13], (L, D_FF, D_MODEL), D_FF ** -0.5),
        'w_in': nrm(ks[14], (L, D_MODEL, N_IN), D_MODEL ** -0.5),
        'conv_w': nrm(ks[15], (L, CONV_WIDTH, D_CONV), CONV_WIDTH ** -0.5),
        'conv_b': nrm(ks[16], (L, D_CONV), 0.02),
        'lambda_q1': nrm(ks[17], (L, HEAD_DIM), 0.1),
        'lambda_k1': nrm(ks[18], (L, HEAD_DIM), 0.1),
        'lambda_q2': nrm(ks[19], (L, HEAD_DIM), 0.1),
        'lambda_k2': nrm(ks[20], (L, HEAD_DIM), 0.1),
        'subln_w': 1.0 + nrm(ks[21], (L, V_DIM), 0.05),
        'w_br_conv': nrm(ks[22], (L, D_CONV, D_MODEL), D_CONV ** -0.5),
        'w_br_attn': nrm(ks[23], (L, V_COLS, D_MODEL), V_COLS ** -0.5),
        'w_out': nrm(ks[24], (L, D_MODEL, D_MODEL), D_MODEL ** -0.5),
        'ffn2_w_gu': nrm(ks[25], (L, D_MODEL, 2 * D_FF), D_MODEL ** -0.5),
        'ffn2_w_down': nrm(ks[26], (L, D_FF, D_MODEL), D_FF ** -0.5),
    }


def reference(x_prompt, x_sample, c_prompt, c_sample, cache_k, cache_v, state_conv, page_table,
              w_ada, b_ada, norm_pre, norm_post, ffn1_w_gu, ffn1_w_down, w_in, conv_w, conv_b,
              lambda_q1, lambda_k1, lambda_q2, lambda_k2, subln_w, w_br_conv, w_br_attn, w_out,
              ffn2_w_gu, ffn2_w_down):
    h_p, h_s = x_prompt, x_sample
    kp_l, vp_l, cp_l, ks_l, vs_l, cs_l = [], [], [], [], [], []
    for l in range(DEPTH):
        p = {
            'w_ada': w_ada[l], 'b_ada': b_ada[l], 'norm_pre': norm_pre[l], 'norm_post': norm_post[l],
            'ffn1_w_gu': ffn1_w_gu[l], 'ffn1_w_down': ffn1_w_down[l], 'w_in': w_in[l],
            'conv_w': conv_w[l], 'conv_b': conv_b[l], 'subln_w': subln_w[l],
            'w_br_conv': w_br_conv[l], 'w_br_attn': w_br_attn[l], 'w_out': w_out[l],
            'ffn2_w_gu': ffn2_w_gu[l], 'ffn2_w_down': ffn2_w_down[l],
        }
        lam_init = 0.8 - 0.6 * math.exp(-0.3 * l)
        f32 = jnp.float32
        lam = (jnp.exp(jnp.sum(lambda_q1[l].astype(f32) * lambda_k1[l].astype(f32)))
               - jnp.exp(jnp.sum(lambda_q2[l].astype(f32) * lambda_k2[l].astype(f32))) + lam_init)
        prev0 = jnp.zeros((h_p.shape[0], CONV_WIDTH - 1, D_CONV), h_p.dtype)
        h_p, cp, kp, vp = _layer(h_p, c_prompt, prev0,
                                 functools.partial(_attend_prompt, lam=lam), p, lam_init)
        h_s, cs, ksm, vs = _layer(h_s, c_sample, state_conv[l],
                                  functools.partial(_attend_sample, cache_k=cache_k, cache_v=cache_v,
                                                    page_table=page_table, layer=l, lam=lam),
                                  p, lam_init)
        kp_l.append(kp); vp_l.append(vp); cp_l.append(cp)
        ks_l.append(ksm); vs_l.append(vs); cs_l.append(cs)
    k_prompt = jnp.stack(kp_l)
    v_prompt = jnp.stack(vp_l)
    conv_prompt = jnp.stack(cp_l)
    k_sample = jnp.stack(ks_l)
    v_sample = jnp.stack(vs_l)
    conv_sample = jnp.stack(cs_l)
    return (h_p, h_s, k_prompt, v_prompt, conv_prompt, k_sample, v_sample, conv_sample)
```

```python
import functools
import math

import jax
import jax.numpy as jnp
from jax import lax
from jax.experimental import pallas as pl
from jax.experimental.pallas import tpu as pltpu

F32 = jnp.float32
BF16 = jnp.bfloat16

D_MODEL = 1024
N_HEADS = 8
HEAD_DIM = 64
V_DIM = 2 * HEAD_DIM
D_CONV = 512
CONV_WIDTH = 3
D_FF = 2816
N_ADA = 9
MACARON_W = 0.5
RMS_EPS = 1e-6
QK_SCALE = HEAD_DIM ** -0.5
N_IN = 3 * D_CONV + 5 * D_MODEL

V7X_LANES = 128
V7X_SUBLANES = 8
V7X_VMEM_BYTES = 64 * 1024 * 1024
VMEM_LIMIT = V7X_VMEM_BYTES - 8 * 1024 * 1024

FF_CHUNK = D_FF // 2
ADA_COLS = N_ADA * D_MODEL // 8
SAMPLE_PAGES_PER_STEP = 8


def _rms(x, g):
    ms = jnp.mean(x * x, axis=-1, keepdims=True)
    return (x * lax.rsqrt(ms + RMS_EPS)) * g


def _mod_slice(mod, j):
    return mod[:, j * D_MODEL:(j + 1) * D_MODEL]


def _modulate(x, g, mod, i):
    return _rms(x, g) * (1.0 + _mod_slice(mod, 3 * i + 1)) + _mod_slice(mod, 3 * i)


def _residual(x, out, g, mod, i, res_w):
    return x + (res_w * _mod_slice(mod, 3 * i + 2)) * _rms(out, g)


def _swiglu(h, wgu_ref, wd_ref):
    acc = None
    for c in range(D_FF // FF_CHUNK):
        lo = c * FF_CHUNK
        g = jnp.dot(h, wgu_ref[:, lo:lo + FF_CHUNK], preferred_element_type=F32)
        u = jnp.dot(h, wgu_ref[:, D_FF + lo:D_FF + lo + FF_CHUNK], preferred_element_type=F32)
        a = (g * jax.nn.sigmoid(g)) * u
        d = jnp.dot(a.astype(BF16), wd_ref[lo:lo + FF_CHUNK, :], preferred_element_type=F32)
        acc = d if acc is None else acc + d
    return acc


def _ada_kernel(c_ref, w_ref, b_ref, o_ref):
    o_ref[...] = jnp.dot(c_ref[...], w_ref[...], preferred_element_type=F32,
                         precision=lax.Precision.HIGHEST) + b_ref[...]


def _ada(c, w, b):
    rows = c.shape[0]
    n = w.shape[1]
    return pl.pallas_call(
        _ada_kernel,
        grid=(n // ADA_COLS,),
        in_specs=[pl.BlockSpec((rows, D_MODEL), lambda j: (0, 0)),
                  pl.BlockSpec((D_MODEL, ADA_COLS), lambda j: (0, j)),
                  pl.BlockSpec((1, ADA_COLS), lambda j: (0, j))],
        out_specs=pl.BlockSpec((rows, ADA_COLS), lambda j: (0, j)),
        out_shape=jax.ShapeDtypeStruct((rows, n), F32),
        compiler_params=pltpu.CompilerParams(dimension_semantics=("parallel",)),
        name="ada",
    )(c, w, b)


def _resident(shape):
    return pl.BlockSpec(shape, lambda *_: (0,) * len(shape), pipeline_mode=pl.Buffered(1))


def _rows(tm, width):
    return pl.BlockSpec((tm, width), lambda i: (i, 0))


def _mod_spec(mod, tm, tiles_per_seq):
    if mod.ndim == 3:
        return pl.BlockSpec((None, 1, mod.shape[-1]), lambda i: (i // tiles_per_seq, 0, 0))
    return pl.BlockSpec((tm, mod.shape[-1]), lambda i: (i, 0))


def _ffn1_kernel(x_ref, mod_ref, npre_ref, npost_ref, wgu_ref, wd_ref, o_ref):
    x = x_ref[...]
    mod = mod_ref[...]
    h = _modulate(x, npre_ref[0:1, :], mod, 0).astype(BF16)
    f = _swiglu(h, wgu_ref, wd_ref)
    o_ref[...] = _residual(x, f, npost_ref[0:1, :], mod, 0, MACARON_W)


def _ffn1(x, mod, npre, npost, wgu, wd, tm, tiles_per_seq):
    n = x.shape[0]
    return pl.pallas_call(
        _ffn1_kernel,
        grid=(n // tm,),
        in_specs=[_rows(tm, D_MODEL), _mod_spec(mod, tm, tiles_per_seq),
                  _resident(npre.shape), _resident(npost.shape),
                  _resident(wgu.shape), _resident(wd.shape)],
        out_specs=_rows(tm, D_MODEL),
        out_shape=jax.ShapeDtypeStruct((n, D_MODEL), F32),
        compiler_params=pltpu.CompilerParams(dimension_semantics=("parallel",),
                                             vmem_limit_bytes=VMEM_LIMIT),
        name="ffn1",
    )(x, mod, npre, npost, wgu, wd)


def _mixin_kernel(*refs, tm, tiles_per_seq, per_row_prev):
    if per_row_prev:
        (x1_ref, mod_ref, npre_ref, win_ref, cw_ref, cb_ref, wbrc_ref, pa_ref, pb_ref,
         q_ref, k32_ref, v32_ref, kb_ref, vb_ref, mc_ref, sg_ref, ut_ref) = refs
    else:
        (x1_ref, mod_ref, npre_ref, win_ref, cw_ref, cb_ref, wbrc_ref,
         q_ref, k32_ref, v32_ref, kb_ref, vb_ref, mc_ref, sg_ref, ut_ref, carry_ref) = refs

    h = _modulate(x1_ref[...], npre_ref[1:2, :], mod_ref[...], 1).astype(BF16)

    def proj(lo, width):
        return jnp.dot(h, win_ref[:, lo:lo + width], preferred_element_type=F32)

    zc = proj(0, 3 * D_CONV)
    xc = zc[:, 0:D_CONV]
    bg = zc[:, D_CONV:2 * D_CONV]
    cg = zc[:, 2 * D_CONV:3 * D_CONV]
    u = cg * xc
    row = lax.broadcasted_iota(jnp.int32, (tm, D_CONV), 0)
    r1 = pltpu.roll(u, 1, 0)
    r2 = pltpu.roll(u, 2, 0)
    if per_row_prev:
        t = row & 3
        um1 = jnp.where(t == 0, pb_ref[...], r1)
        um2 = jnp.where(t == 0, pa_ref[...], jnp.where(t == 1, pb_ref[...], r2))
    else:
        @pl.when(pl.program_id(0) % tiles_per_seq == 0)
        def _():
            carry_ref[...] = jnp.zeros_like(carry_ref)
        c6 = carry_ref[6:7, :]
        c7 = carry_ref[7:8, :]
        um1 = jnp.where(row == 0, c7, r1)
        um2 = jnp.where(row == 0, c6, jnp.where(row == 1, c7, r2))
        carry_ref[...] = u[tm - V7X_SUBLANES:tm, :]
    conv = cb_ref[...] + cw_ref[0:1, :] * um2
    conv = conv + cw_ref[1:2, :] * um1
    conv = conv + cw_ref[2:3, :] * u
    tail = ut_ref.shape[0]
    ut_ref[...] = u[tm - tail:tm, :]
    yc = (bg * conv).astype(BF16)

    q = proj(3 * D_CONV, D_MODEL)
    q_ref[...] = (q * QK_SCALE).astype(BF16)
    k = proj(3 * D_CONV + D_MODEL, D_MODEL)
    k32_ref[...] = k
    kb_ref[...] = k.astype(BF16)
    v = proj(3 * D_CONV + 2 * D_MODEL, D_MODEL)
    v32_ref[...] = v
    vb_ref[...] = v.astype(BF16)
    gc = proj(3 * D_CONV + 3 * D_MODEL, D_MODEL)
    mc_ref[...] = jax.nn.sigmoid(gc) * jnp.dot(yc, wbrc_ref[...], preferred_element_type=F32)
    ga = proj(3 * D_CONV + 4 * D_MODEL, D_MODEL)
    sg_ref[...] = jax.nn.sigmoid(ga)


def _mixin(x1, mod, npre, win, cw, cb, wbrc, prev_rows, tm, tiles_per_seq):
    n = x1.shape[0]
    per_row_prev = prev_rows is not None
    tail = tm if per_row_prev else V7X_SUBLANES
    in_specs = [_rows(tm, D_MODEL), _mod_spec(mod, tm, tiles_per_seq), _resident(npre.shape),
                _resident(win.shape), _resident(cw.shape), _resident(cb.shape), _resident(wbrc.shape)]
    args = [x1, mod, npre, win, cw, cb, wbrc]
    scratch = []
    if per_row_prev:
        in_specs += [_rows(tm, D_CONV), _rows(tm, D_CONV)]
        args += list(prev_rows)
    else:
        scratch = [pltpu.VMEM((V7X_SUBLANES, D_CONV), F32)]
    wide = lambda dt: jax.ShapeDtypeStruct((n, D_MODEL), dt)
    return pl.pallas_call(
        functools.partial(_mixin_kernel, tm=tm, tiles_per_seq=tiles_per_seq, per_row_prev=per_row_prev),
        grid=(n // tm,),
        in_specs=in_specs,
        out_specs=[_rows(tm, D_MODEL)] * 7 + [pl.BlockSpec((None, tail, D_CONV), lambda i: (i, 0, 0))],
        out_shape=[wide(BF16), wide(F32), wide(F32), wide(BF16), wide(BF16), wide(F32), wide(F32),
                   jax.ShapeDtypeStruct((n // tm, tail, D_CONV), F32)],
        scratch_shapes=scratch,
        compiler_params=pltpu.CompilerParams(dimension_semantics=("arbitrary",),
                                             vmem_limit_bytes=VMEM_LIMIT),
        name="mixin",
    )(*args)


def _mixout_kernel(x1_ref, mc_ref, sg_ref, ya_ref, mod_ref, npre_ref, npost_ref,
                   wbra_ref, wout_ref, wgu_ref, wd_ref, o_ref):
    mod = mod_ref[...]
    att = jnp.dot(ya_ref[...].astype(BF16), wbra_ref[...], preferred_element_type=F32)
    merged = mc_ref[...] + sg_ref[...] * att
    mix = jnp.dot(merged.astype(BF16), wout_ref[...], preferred_element_type=F32)
    x2 = _residual(x1_ref[...], mix, npost_ref[1:2, :], mod, 1, 1.0)
    h = _modulate(x2, npre_ref[2:3, :], mod, 2).astype(BF16)
    f = _swiglu(h, wgu_ref, wd_ref)
    o_ref[...] = _residual(x2, f, npost_ref[2:3, :], mod, 2, MACARON_W)


def _mixout(x1, mc, sg, ya, mod, npre, npost, wbra, wout, wgu, wd, tm, tiles_per_seq):
    n = x1.shape[0]
    return pl.pallas_call(
        _mixout_kernel,
        grid=(n // tm,),
        in_specs=[_rows(tm, D_MODEL)] * 4 + [_mod_spec(mod, tm, tiles_per_seq)]
        + [_resident(a.shape) for a in (npre, npost, wbra, wout, wgu, wd)],
        out_specs=_rows(tm, D_MODEL),
        out_shape=jax.ShapeDtypeStruct((n, D_MODEL), F32),
        compiler_params=pltpu.CompilerParams(dimension_semantics=("parallel",),
                                             vmem_limit_bytes=VMEM_LIMIT),
        name="mixout",
    )(x1, mc, sg, ya, mod, npre, npost, wbra, wout, wgu, wd)


def _lambda(lam_ref, lam_init):
    lv = lam_ref[...]
    s1 = jnp.sum(lv[0:1, :] * lv[1:2, :], axis=-1, keepdims=True)
    s2 = jnp.sum(lv[2:3, :] * lv[3:4, :], axis=-1, keepdims=True)
    return jnp.exp(s1) - jnp.exp(s2) + lam_init


def _head_out(o, sw, lam_init):
    return _rms(o, sw) * (1.0 - lam_init)


def _attn_prompt_kernel(lam_ref, q_ref, k_ref, v_ref, sw_ref, o_ref, m_ref, l_ref, acc_ref,
                        *, tq, lam_init):
    qi = pl.program_id(2)
    q = q_ref[...]
    lane = lax.broadcasted_iota(jnp.int32, (tq, V_DIM), 1)
    zero = jnp.zeros_like(q)
    qs = jnp.concatenate([jnp.where(lane < HEAD_DIM, q, zero),
                          jnp.where(lane >= HEAD_DIM, q, zero)], axis=0)
    m_ref[...] = jnp.full_like(m_ref, -jnp.inf)
    l_ref[...] = jnp.zeros_like(l_ref)
    acc_ref[...] = jnp.zeros_like(acc_ref)

    def step(kj, diagonal):
        start = pl.multiple_of(kj * tq, tq)
        kb = k_ref[pl.ds(start, tq), :]
        vb = v_ref[pl.ds(start, tq), :]
        s = lax.dot_general(qs, kb, (((1,), (1,)), ((), ())), preferred_element_type=F32)
        if diagonal:
            r = lax.broadcasted_iota(jnp.int32, (2 * tq, tq), 0)
            c = lax.broadcasted_iota(jnp.int32, (2 * tq, tq), 1)
            r = jnp.where(r >= tq, r - tq, r)
            s = jnp.where(c <= r, s, -jnp.inf)
        m_prev = m_ref[...]
        m_new = jnp.maximum(m_prev, jnp.max(s, axis=-1, keepdims=True))
        alpha = jnp.exp(m_prev - m_new)
        p = jnp.exp(s - m_new)
        l_ref[...] = alpha * l_ref[...] + jnp.sum(p, axis=-1, keepdims=True)
        acc_ref[...] = alpha * acc_ref[...] + jnp.dot(p.astype(BF16), vb, preferred_element_type=F32)
        m_ref[...] = m_new

    def body(kj, carry):
        step(kj, False)
        return carry

    lax.fori_loop(0, qi, body, 0)
    step(qi, True)

    lam = _lambda(lam_ref, lam_init)
    on = acc_ref[...] / l_ref[...]
    o = on[0:tq, :] - lam * on[tq:2 * tq, :]
    o_ref[...] = _head_out(o, sw_ref[...], lam_init).astype(o_ref.dtype)


def _attn_prompt(lam_vecs, q, k, v, sw, tq, lam_init):
    b, s, _ = q.shape
    qo_spec = pl.BlockSpec((None, tq, V_DIM), lambda bi, h, i: (bi, i, h))
    kv_spec = pl.BlockSpec((None, s, V_DIM), lambda bi, h, i: (bi, 0, h))
    return pl.pallas_call(
        functools.partial(_attn_prompt_kernel, tq=tq, lam_init=lam_init),
        grid=(b, N_HEADS, s // tq),
        in_specs=[pl.BlockSpec(lam_vecs.shape, lambda bi, h, i: (0, 0)), qo_spec, kv_spec, kv_spec,
                  pl.BlockSpec(sw.shape, lambda bi, h, i: (0, 0))],
        out_specs=qo_spec,
        out_shape=jax.ShapeDtypeStruct((b, s, D_MODEL), BF16),
        scratch_shapes=[pltpu.VMEM((2 * tq, 1), F32), pltpu.VMEM((2 * tq, 1), F32),
                        pltpu.VMEM((2 * tq, V_DIM), F32)],
        compiler_params=pltpu.CompilerParams(dimension_semantics=("parallel", "parallel", "arbitrary"),
                                             vmem_limit_bytes=VMEM_LIMIT),
        name="attn_prompt",
    )(lam_vecs, q, k, v, sw)


def _attn_sample_kernel(pt_ref, lam_ref, q_ref, kn_ref, vn_ref, sw_ref, *refs, pages, lam_init):
    k_refs = refs[:pages]
    v_refs = refs[pages:2 * pages]
    o_ref, m_ref, l_ref, acc_ref = refs[2 * pages:]
    j = pl.program_id(1)
    rows = q_ref.shape[0]

    @pl.when(j == 0)
    def _():
        m_ref[...] = jnp.full_like(m_ref, -jnp.inf)
        l_ref[...] = jnp.zeros_like(l_ref)
        acc_ref[...] = jnp.zeros_like(acc_ref)

    rgrp = lax.broadcasted_iota(jnp.int32, (rows, D_MODEL), 0) // 4
    cgrp = lax.broadcasted_iota(jnp.int32, (rows, D_MODEL), 1) // HEAD_DIM
    q = q_ref[...]
    qbd = jnp.where(rgrp == cgrp, q, jnp.zeros_like(q))

    s = jnp.concatenate(
        [lax.dot_general(qbd, k_refs[i][...].astype(BF16), (((1,), (1,)), ((), ())),
                         preferred_element_type=F32) for i in range(pages)], axis=1)
    m_prev = m_ref[...]
    m_new = jnp.maximum(m_prev, jnp.max(s, axis=-1, keepdims=True))
    alpha = jnp.exp(m_prev - m_new)
    p = jnp.exp(s - m_new)
    l_ref[...] = alpha * l_ref[...] + jnp.sum(p, axis=-1, keepdims=True)
    pv = None
    for i in range(pages):
        d = jnp.dot(p[:, i * 128:(i + 1) * 128].astype(BF16), v_refs[i][...].astype(BF16),
                    preferred_element_type=F32)
        pv = d if pv is None else pv + d
    acc_ref[...] = alpha * acc_ref[...] + pv
    m_ref[...] = m_new

    @pl.when(j == pl.num_programs(1) - 1)
    def _():
        qf = qbd.astype(F32)
        t = lax.broadcasted_iota(jnp.int32, (rows, 1), 0) & 3
        n_new = kn_ref.shape[0]
        sn = []
        for tk in range(n_new):
            st = jnp.sum(qf * kn_ref[tk:tk + 1, :], axis=-1, keepdims=True)
            sn.append(jnp.where(t >= tk, st, -jnp.inf))
        m_prev = m_ref[...]
        m_new = m_prev
        for st in sn:
            m_new = jnp.maximum(m_new, st)
        alpha = jnp.exp(m_prev - m_new)
        l = alpha * l_ref[...]
        acc = alpha * acc_ref[...]
        for tk in range(n_new):
            pt = jnp.exp(sn[tk] - m_new)
            l = l + pt
            acc = acc + pt * vn_ref[tk:tk + 1, :]
        on = acc / l
        lam = _lambda(lam_ref, lam_init)
        for h in range(N_HEADS):
            a = on[8 * h:8 * h + 8, V_DIM * h:V_DIM * (h + 1)]
            o = a - lam * pltpu.roll(a, 4, 0)
            o_ref[:, V_DIM * h:V_DIM * (h + 1)] = _head_out(o, sw_ref[...], lam_init)


def _attn_sample(page_table, lam_vecs, qrep, k_new, v_new, sw, cache_k, cache_v, lam_init):
    n_seq, n_pages = page_table.shape
    pages = min(SAMPLE_PAGES_PER_STEP, n_pages)
    rows = qrep.shape[1]

    def page_spec(i):
        return pl.BlockSpec((None, cache_k.shape[1], D_MODEL),
                            lambda s, j, pt: (pt[s, j * pages + i], 0, 0))

    per_seq = lambda a: pl.BlockSpec((None,) + a.shape[1:], lambda s, j, pt: (s, 0, 0))
    grid_spec = pltpu.PrefetchScalarGridSpec(
        num_scalar_prefetch=1,
        grid=(n_seq, n_pages // pages),
        in_specs=[pl.BlockSpec(lam_vecs.shape, lambda s, j, pt: (0, 0)),
                  per_seq(qrep), per_seq(k_new), per_seq(v_new),
                  pl.BlockSpec(sw.shape, lambda s, j, pt: (0, 0))]
        + [page_spec(i) for i in range(pages)] * 2,
        out_specs=pl.BlockSpec((None, V7X_SUBLANES, D_MODEL), lambda s, j, pt: (s, 0, 0)),
        scratch_shapes=[pltpu.VMEM((rows, 1), F32), pltpu.VMEM((rows, 1), F32),
                        pltpu.VMEM((rows, D_MODEL), F32)],
    )
    return pl.pallas_call(
        functools.partial(_attn_sample_kernel, pages=pages, lam_init=lam_init),
        grid_spec=grid_spec,
        out_shape=jax.ShapeDtypeStruct((n_seq, V7X_SUBLANES, D_MODEL), F32),
        compiler_params=pltpu.CompilerParams(dimension_semantics=("parallel", "arbitrary"),
                                             vmem_limit_bytes=VMEM_LIMIT),
        name="attn_sample",
    )(page_table, lam_vecs, qrep, k_new, v_new, sw, *([cache_k] * pages), *([cache_v] * pages))


def _tile(n, cap):
    t = cap
    while n % t:
        t //= 2
    return t


def kernel(x_prompt, x_sample, c_prompt, c_sample, cache_k, cache_v, state_conv, page_table, w_ada, b_ada, norm_pre, norm_post, ffn1_w_gu, ffn1_w_down, w_in, conv_w, conv_b, lambda_q1, lambda_k1, lambda_q2, lambda_k2, subln_w, w_br_conv, w_br_attn, w_out, ffn2_w_gu, ffn2_w_down):
    depth = w_ada.shape[0]
    assert depth == 1, "single-layer step"
    batch, seq, _ = x_prompt.shape
    dec_batch, dec_seq, _ = x_sample.shape
    assert dec_seq == CONV_WIDTH + 1
    l = 0
    lam_init = 0.8 - 0.6 * math.exp(-0.3 * l)

    n_c = batch + dec_batch
    c_all = jnp.concatenate([c_prompt, c_sample], axis=0)
    c_all = jnp.pad(c_all, ((0, -n_c % V7X_SUBLANES), (0, 0)))
    mod_all = _ada(c_all, w_ada[l], b_ada[l].reshape(1, -1))
    mod_p = mod_all[:batch].reshape(batch, 1, -1)
    mod_s = jnp.repeat(mod_all[batch:n_c], dec_seq, axis=0)

    bf = lambda w: w[l].astype(BF16)
    wgu1, wd1, win, wbrc, wbra, wout, wgu2, wd2 = map(
        bf, (ffn1_w_gu, ffn1_w_down, w_in, w_br_conv, w_br_attn, w_out, ffn2_w_gu, ffn2_w_down))
    npre, npost = norm_pre[l], norm_post[l]
    cw, cb = conv_w[l], conv_b[l].reshape(1, -1)
    sw = subln_w[l].reshape(1, -1)
    lam_vecs = jnp.stack([lambda_q1[l], lambda_k1[l], lambda_q2[l], lambda_k2[l]])

    n_p = batch * seq
    tm_f = _tile(seq, 512)
    tm_m = _tile(seq, 256)
    xp = x_prompt.reshape(n_p, D_MODEL)
    x1p = _ffn1(xp, mod_p, npre, npost, wgu1, wd1, tm_f, seq // tm_f)
    qp, k32p, v32p, kbp, vbp, mcp, sgp, utp = _mixin(
        x1p, mod_p, npre, win, cw, cb, wbrc, None, tm_m, seq // tm_m)
    tq = _tile(seq, 256)
    as3 = lambda a: a.reshape(batch, seq, D_MODEL)
    yap = _attn_prompt(lam_vecs, as3(qp), as3(kbp), as3(vbp), sw, tq, lam_init)
    yp = _mixout(x1p, mcp, sgp, yap.reshape(n_p, D_MODEL), mod_p, npre, npost,
                 wbra, wout, wgu2, wd2, tm_f, seq // tm_f)

    n_s = dec_batch * dec_seq
    tm_s = _tile(n_s, 128)
    xs = x_sample.reshape(n_s, D_MODEL)
    prev_rows = (jnp.repeat(state_conv[l, :, 0], dec_seq, axis=0),
                 jnp.repeat(state_conv[l, :, 1], dec_seq, axis=0))
    x1s = _ffn1(xs, mod_s, npre, npost, wgu1, wd1, tm_s, 1)
    qs, k32s, v32s, _, _, mcs, sgs, uts = _mixin(
        x1s, mod_s, npre, win, cw, cb, wbrc, prev_rows, tm_s, 1)
    qrep = jnp.tile(qs.reshape(dec_batch, dec_seq, D_MODEL), (1, 2 * N_HEADS, 1))
    n_pool, page = cache_k.shape[1], cache_k.shape[2]
    yas = _attn_sample(page_table, lam_vecs, qrep,
                       k32s.reshape(dec_batch, dec_seq, D_MODEL), v32s.reshape(dec_batch, dec_seq, D_MODEL),
                       sw, cache_k[l].reshape(n_pool, page, D_MODEL), cache_v[l].reshape(n_pool, page, D_MODEL),
                       lam_init)
    yas = yas[:, :dec_seq].reshape(n_s, D_MODEL)
    ys = _mixout(x1s, mcs, sgs, yas, mod_s, npre, npost, wbra, wout, wgu2, wd2, tm_s, 1)

    tiles = seq // tm_m
    conv_prompt = utp.reshape(batch, tiles, V7X_SUBLANES, D_CONV)[:, -1, -(CONV_WIDTH - 1):]
    conv_sample = uts.reshape(dec_batch, dec_seq, D_CONV)[:, -(CONV_WIDTH - 1):]
    return (yp.reshape(batch, seq, D_MODEL),
            ys.reshape(dec_batch, dec_seq, D_MODEL),
            k32p.reshape(1, batch, seq, N_HEADS, 2, HEAD_DIM),
            v32p.reshape(1, batch, seq, N_HEADS, V_DIM),
            conv_prompt[None],
            k32s.reshape(1, dec_batch, dec_seq, N_HEADS, 2, HEAD_DIM),
            v32s.reshape(1, dec_batch, dec_seq, N_HEADS, V_DIM),
            conv_sample[None])
```

```python
import functools
import math

import jax
import jax.numpy as jnp
from jax import lax
from jax.experimental import pallas as pl
from jax.experimental.pallas import tpu as pltpu

F32 = jnp.float32
BF16 = jnp.bfloat16

D_MODEL = 1024
N_HEADS = 8
HEAD_DIM = 64
V_DIM = 2 * HEAD_DIM
D_CONV = 512
CONV_WIDTH = 3
D_FF = 2816
N_ADA = 9
MACARON_W = 0.5
RMS_EPS = 1e-6
QK_SCALE = HEAD_DIM ** -0.5
LOG2_E = math.log2(math.e)

V7X_LANES = 128
V7X_SUBLANES = 8
V7X_VMEM_BYTES = 64 * 1024 * 1024
VMEM_LIMIT = V7X_VMEM_BYTES - 8 * 1024 * 1024

FF_CHUNK = D_FF // 2
ADA_COLS = N_ADA * D_MODEL // 8
ROWS_PER_HEAD = 2 * (CONV_WIDTH + 1)
SAMPLE_PAGES_PER_STEP = 16


def _rms(x, g, axis=-1):
    ms = jnp.mean(x * x, axis=axis, keepdims=True)
    return (x * lax.rsqrt(ms + RMS_EPS)) * g


def _mod_slice(mod, j):
    return mod[:, j * D_MODEL:(j + 1) * D_MODEL]


def _modulate(x, g, mod, i):
    return _rms(x, g) * (1.0 + _mod_slice(mod, 3 * i + 1)) + _mod_slice(mod, 3 * i)


def _residual(x, out, g, mod, i, res_w):
    return x + (res_w * _mod_slice(mod, 3 * i + 2)) * _rms(out, g)


def _swiglu(h, wgu_ref, wd_ref):
    acc = None
    for c in range(D_FF // FF_CHUNK):
        lo = c * FF_CHUNK
        g = jnp.dot(h, wgu_ref[:, lo:lo + FF_CHUNK], preferred_element_type=F32)
        u = jnp.dot(h, wgu_ref[:, D_FF + lo:D_FF + lo + FF_CHUNK], preferred_element_type=F32)
        a = (g * jax.nn.sigmoid(g)) * u
        d = jnp.dot(a.astype(BF16), wd_ref[lo:lo + FF_CHUNK, :], preferred_element_type=F32)
        acc = d if acc is None else acc + d
    return acc


def _ada_kernel(c_ref, w_ref, b_ref, o_ref):
    o_ref[...] = jnp.dot(c_ref[...], w_ref[...], preferred_element_type=F32,
                         precision=lax.Precision.HIGHEST) + b_ref[...]


def _ada(c, w, b):
    rows = c.shape[0]
    n = w.shape[1]
    return pl.pallas_call(
        _ada_kernel,
        grid=(n // ADA_COLS,),
        in_specs=[pl.BlockSpec((rows, D_MODEL), lambda j: (0, 0)),
                  pl.BlockSpec((D_MODEL, ADA_COLS), lambda j: (0, j)),
                  pl.BlockSpec((1, ADA_COLS), lambda j: (0, j))],
        out_specs=pl.BlockSpec((rows, ADA_COLS), lambda j: (0, j)),
        out_shape=jax.ShapeDtypeStruct((rows, n), F32),
        compiler_params=pltpu.CompilerParams(dimension_semantics=("parallel",)),
        name="ada",
    )(c, w, b)


def _resident(shape):
    return pl.BlockSpec(shape, lambda *_: (0,) * len(shape), pipeline_mode=pl.Buffered(1))


def _rows(tm, width):
    return pl.BlockSpec((tm, width), lambda i: (i, 0))


def _mod_spec(mod, tm, tiles_per_seq):
    if mod.ndim == 3:
        return pl.BlockSpec((None, 1, mod.shape[-1]), lambda i: (i // tiles_per_seq, 0, 0))
    return pl.BlockSpec((tm, mod.shape[-1]), lambda i: (i, 0))


def _ffn1_kernel(x_ref, mod_ref, npre_ref, npost_ref, wgu_ref, wd_ref, o_ref):
    x = x_ref[...]
    mod = mod_ref[...]
    h = _modulate(x, npre_ref[0:1, :], mod, 0).astype(BF16)
    f = _swiglu(h, wgu_ref, wd_ref)
    o_ref[...] = _residual(x, f, npost_ref[0:1, :], mod, 0, MACARON_W)


def _ffn1(x, mod, npre, npost, wgu, wd, tm, tiles_per_seq):
    n = x.shape[0]
    return pl.pallas_call(
        _ffn1_kernel,
        grid=(n // tm,),
        in_specs=[_rows(tm, D_MODEL), _mod_spec(mod, tm, tiles_per_seq),
                  _resident(npre.shape), _resident(npost.shape),
                  _resident(wgu.shape), _resident(wd.shape)],
        out_specs=_rows(tm, D_MODEL),
        out_shape=jax.ShapeDtypeStruct((n, D_MODEL), F32),
        compiler_params=pltpu.CompilerParams(dimension_semantics=("parallel",),
                                             vmem_limit_bytes=VMEM_LIMIT),
        name="ffn1",
    )(x, mod, npre, npost, wgu, wd)


def _mixin_kernel(*refs, tm, tiles_per_seq, sample):
    if sample:
        (x1_ref, mod_ref, npre_ref, win_ref, cw_ref, cb_ref, wbrc_ref, pa_ref, pb_ref,
         q_ref, k32_ref, v32_ref, mc_ref, sg_ref, ut_ref) = refs
    else:
        (x1_ref, mod_ref, npre_ref, win_ref, cw_ref, cb_ref, wbrc_ref,
         q_ref, kb_ref, vt_ref, k32_ref, v32_ref, mc_ref, sg_ref, ut_ref, carry_ref) = refs

    h = _modulate(x1_ref[...], npre_ref[1:2, :], mod_ref[...], 1).astype(BF16)

    def proj(lo, width):
        return jnp.dot(h, win_ref[:, lo:lo + width], preferred_element_type=F32)

    zc = proj(0, 3 * D_CONV)
    xc = zc[:, 0:D_CONV]
    bg = zc[:, D_CONV:2 * D_CONV]
    cg = zc[:, 2 * D_CONV:3 * D_CONV]
    u = cg * xc
    row = lax.broadcasted_iota(jnp.int32, (tm, D_CONV), 0)
    r1 = pltpu.roll(u, 1, 0)
    r2 = pltpu.roll(u, 2, 0)
    if sample:
        t = row & CONV_WIDTH
        um1 = jnp.where(t == 0, pb_ref[...], r1)
        um2 = jnp.where(t == 0, pa_ref[...], jnp.where(t == 1, pb_ref[...], r2))
    else:
        @pl.when(pl.program_id(0) % tiles_per_seq == 0)
        def _():
            carry_ref[...] = jnp.zeros_like(carry_ref)
        c6 = carry_ref[6:7, :]
        c7 = carry_ref[7:8, :]
        um1 = jnp.where(row == 0, c7, r1)
        um2 = jnp.where(row == 0, c6, jnp.where(row == 1, c7, r2))
        carry_ref[...] = u[tm - V7X_SUBLANES:tm, :]
    conv = cb_ref[...] + cw_ref[0:1, :] * um2
    conv = conv + cw_ref[1:2, :] * um1
    conv = conv + cw_ref[2:3, :] * u
    tail = ut_ref.shape[0]
    ut_ref[...] = u[tm - tail:tm, :]
    yc = (bg * conv).astype(BF16)

    q = (proj(3 * D_CONV, D_MODEL) * (QK_SCALE if sample else QK_SCALE * LOG2_E)).astype(BF16)
    k = proj(3 * D_CONV + D_MODEL, D_MODEL)
    k32_ref[...] = k
    v = proj(3 * D_CONV + 2 * D_MODEL, D_MODEL)
    v32_ref[...] = v
    if sample:
        q_ref[...] = q
    else:
        kb = k.astype(BF16)
        for hd in range(N_HEADS):
            cols = slice(hd * V_DIM, (hd + 1) * V_DIM)
            q_ref[hd] = q[:, cols]
            kb_ref[hd] = kb[:, cols]
            vt_ref[hd] = v[:, cols].T.astype(BF16)
    gc = proj(3 * D_CONV + 3 * D_MODEL, D_MODEL)
    mc_ref[...] = jax.nn.sigmoid(gc) * jnp.dot(yc, wbrc_ref[...], preferred_element_type=F32)
    ga = proj(3 * D_CONV + 4 * D_MODEL, D_MODEL)
    sg_ref[...] = jax.nn.sigmoid(ga)


def _mixin(x1, mod, npre, win, cw, cb, wbrc, prev_rows, tm, seq):
    n = x1.shape[0]
    sample = prev_rows is not None
    tiles_per_seq = 1 if sample else seq // tm
    in_specs = [_rows(tm, D_MODEL), _mod_spec(mod, tm, tiles_per_seq), _resident(npre.shape),
                _resident(win.shape), _resident(cw.shape), _resident(cb.shape), _resident(wbrc.shape)]
    args = [x1, mod, npre, win, cw, cb, wbrc]
    wide = lambda dt: jax.ShapeDtypeStruct((n, D_MODEL), dt)
    if sample:
        in_specs += [_rows(tm, D_CONV), _rows(tm, D_CONV)]
        args += list(prev_rows)
        scratch = []
        tail = tm
        head_specs = [_rows(tm, D_MODEL)]
        head_shapes = [wide(BF16)]
    else:
        scratch = [pltpu.VMEM((V7X_SUBLANES, D_CONV), F32)]
        tail = V7X_SUBLANES
        b = n // seq
        hm_spec = pl.BlockSpec((None, N_HEADS, tm, V_DIM),
                               lambda i: (i // tiles_per_seq, 0, i % tiles_per_seq, 0))
        vt_spec = pl.BlockSpec((None, N_HEADS, None, V_DIM, tm),
                               lambda i: (i // tiles_per_seq, 0, i % tiles_per_seq, 0, 0))
        head_specs = [hm_spec, hm_spec, vt_spec]
        head_shapes = [jax.ShapeDtypeStruct((b, N_HEADS, seq, V_DIM), BF16)] * 2 + [
            jax.ShapeDtypeStruct((b, N_HEADS, tiles_per_seq, V_DIM, tm), BF16)]
    return pl.pallas_call(
        functools.partial(_mixin_kernel, tm=tm, tiles_per_seq=tiles_per_seq, sample=sample),
        grid=(n // tm,),
        in_specs=in_specs,
        out_specs=head_specs + [_rows(tm, D_MODEL)] * 4
        + [pl.BlockSpec((None, tail, D_CONV), lambda i: (i, 0, 0))],
        out_shape=head_shapes + [wide(F32)] * 4 + [jax.ShapeDtypeStruct((n // tm, tail, D_CONV), F32)],
        scratch_shapes=scratch,
        compiler_params=pltpu.CompilerParams(dimension_semantics=("arbitrary",),
                                             vmem_limit_bytes=VMEM_LIMIT),
        name="mixin",
    )(*args)


def _mixout_kernel(x1_ref, mc_ref, sg_ref, ya_ref, mod_ref, npre_ref, npost_ref,
                   wbra_ref, wout_ref, wgu_ref, wd_ref, o_ref):
    mod = mod_ref[...]
    att = jnp.dot(ya_ref[...].astype(BF16), wbra_ref[...], preferred_element_type=F32)
    merged = mc_ref[...] + sg_ref[...] * att
    mix = jnp.dot(merged.astype(BF16), wout_ref[...], preferred_element_type=F32)
    x2 = _residual(x1_ref[...], mix, npost_ref[1:2, :], mod, 1, 1.0)
    h = _modulate(x2, npre_ref[2:3, :], mod, 2).astype(BF16)
    f = _swiglu(h, wgu_ref, wd_ref)
    o_ref[...] = _residual(x2, f, npost_ref[2:3, :], mod, 2, MACARON_W)


def _mixout(x1, mc, sg, ya, mod, npre, npost, wbra, wout, wgu, wd, tm, tiles_per_seq):
    n = x1.shape[0]
    return pl.pallas_call(
        _mixout_kernel,
        grid=(n // tm,),
        in_specs=[_rows(tm, D_MODEL)] * 4 + [_mod_spec(mod, tm, tiles_per_seq)]
        + [_resident(a.shape) for a in (npre, npost, wbra, wout, wgu, wd)],
        out_specs=_rows(tm, D_MODEL),
        out_shape=jax.ShapeDtypeStruct((n, D_MODEL), F32),
        compiler_params=pltpu.CompilerParams(dimension_semantics=("parallel",),
                                             vmem_limit_bytes=VMEM_LIMIT),
        name="mixout",
    )(x1, mc, sg, ya, mod, npre, npost, wbra, wout, wgu, wd)


def _lambda(lam_ref, lam_init):
    lv = lam_ref[...]
    s1 = jnp.sum(lv[0:1, :] * lv[1:2, :], axis=-1, keepdims=True)
    s2 = jnp.sum(lv[2:3, :] * lv[3:4, :], axis=-1, keepdims=True)
    return jnp.exp(s1) - jnp.exp(s2) + lam_init


ATTN_COL_CHUNK = 256


def _attn_prompt_kernel(lam_ref, q_ref, k_ref, vt_ref, sw_ref, o_ref, qs_ref, st_ref, m_ref, l_ref, acc_ref,
                        *, tq, lam_init):
    qi = pl.program_id(2)
    vblk = vt_ref.shape[2]
    nsub = tq // vblk
    cw = ATTN_COL_CHUNK
    q = q_ref[...]
    lane = lax.broadcasted_iota(jnp.int32, (tq, V_DIM), 1)
    zero = jnp.zeros_like(q)
    qs_ref[0:tq, :] = jnp.where(lane < HEAD_DIM, q, zero)
    qs_ref[tq:2 * tq, :] = jnp.where(lane >= HEAD_DIM, q, zero)
    m_ref[...] = jnp.full_like(m_ref, -jnp.inf)
    l_ref[...] = jnp.zeros_like(l_ref)
    acc_ref[...] = jnp.zeros_like(acc_ref)

    def scores(kj, slot):
        kb = k_ref[pl.ds(pl.multiple_of(kj * tq, tq), tq), :]
        for c in range(2 * tq // cw):
            st_ref[slot, :, c * cw:(c + 1) * cw] = lax.dot_general(
                kb, qs_ref[c * cw:(c + 1) * cw, :], (((1,), (1,)), ((), ())), preferred_element_type=F32)

    def consume(kj, slot, diagonal):
        for c in range(2 * tq // cw):
            cols = slice(c * cw, (c + 1) * cw)
            st = st_ref[slot, :, cols]
            if diagonal:
                key = lax.broadcasted_iota(jnp.int32, (tq, cw), 0)
                qry = lax.broadcasted_iota(jnp.int32, (tq, cw), 1) + (c * cw) % tq
                st = jnp.where(key <= qry, st, -jnp.inf)
            m_prev = m_ref[:, cols]
            m_new = jnp.maximum(m_prev, jnp.max(st, axis=0, keepdims=True))
            alpha = jnp.exp2(m_prev - m_new)
            pt = jnp.exp2(st - m_new)
            l_ref[:, cols] = alpha * l_ref[:, cols] + jnp.sum(pt, axis=0, keepdims=True)
            vt = jnp.concatenate([vt_ref[kj * nsub + sub] for sub in range(nsub)], axis=1)
            pv = jnp.dot(vt, pt.astype(BF16), preferred_element_type=F32)
            acc_ref[:, cols] = alpha * acc_ref[:, cols] + pv
            m_ref[:, cols] = m_new

    scores(0, 0)

    def body(i, carry):
        kj = 2 * i
        scores(kj + 1, 1)
        consume(kj, 0, False)
        scores(kj + 2, 0)
        consume(kj + 1, 1, False)
        return carry

    lax.fori_loop(0, qi // 2, body, 0)

    @pl.when(qi % 2 == 1)
    def _():
        scores(qi, 1)
        consume(qi - 1, 0, False)
        consume(qi, 1, True)

    @pl.when(qi % 2 == 0)
    def _():
        consume(qi, 0, True)

    lam = _lambda(lam_ref, lam_init)
    on = acc_ref[...] / l_ref[...]
    ot = on[:, 0:tq] - lam * on[:, tq:2 * tq]
    yt = _rms(ot, sw_ref[...], axis=0) * (1.0 - lam_init)
    o_ref[...] = yt.T.astype(o_ref.dtype)


def _attn_prompt(lam_vecs, q, k, vt, sw_col, tq, lam_init):
    b, _, s, _ = q.shape
    nblk, vblk = vt.shape[2], vt.shape[4]
    return pl.pallas_call(
        functools.partial(_attn_prompt_kernel, tq=tq, lam_init=lam_init),
        grid=(b, N_HEADS, s // tq),
        in_specs=[pl.BlockSpec(lam_vecs.shape, lambda bi, h, i: (0, 0)),
                  pl.BlockSpec((None, None, tq, V_DIM), lambda bi, h, i: (bi, h, i, 0)),
                  pl.BlockSpec((None, None, s, V_DIM), lambda bi, h, i: (bi, h, 0, 0)),
                  pl.BlockSpec((None, None, nblk, V_DIM, vblk), lambda bi, h, i: (bi, h, 0, 0, 0)),
                  pl.BlockSpec(sw_col.shape, lambda bi, h, i: (0, 0))],
        out_specs=pl.BlockSpec((None, tq, V_DIM), lambda bi, h, i: (bi, i, h)),
        out_shape=jax.ShapeDtypeStruct((b, s, D_MODEL), BF16),
        scratch_shapes=[pltpu.VMEM((2 * tq, V_DIM), BF16), pltpu.VMEM((2, tq, 2 * tq), F32),
                        pltpu.VMEM((1, 2 * tq), F32), pltpu.VMEM((1, 2 * tq), F32),
                        pltpu.VMEM((V_DIM, 2 * tq), F32)],
        compiler_params=pltpu.CompilerParams(dimension_semantics=("parallel", "parallel", "arbitrary"),
                                             vmem_limit_bytes=VMEM_LIMIT),
        name="attn_prompt",
    )(lam_vecs, q, k, vt, sw_col)


def _attn_sample_kernel(pt_ref, lam_ref, q_ref, kn_ref, vn_ref, sw_ref, *refs, pages, lam_init):
    k_refs = refs[:pages]
    v_refs = refs[pages:2 * pages]
    o_ref, m_ref, l_ref, acc_ref = refs[2 * pages:]
    j = pl.program_id(1)
    rows = q_ref.shape[0]
    page = k_refs[0].shape[1]
    rph = ROWS_PER_HEAD

    @pl.when(j == 0)
    def _():
        m_ref[...] = jnp.full_like(m_ref, -jnp.inf)
        l_ref[...] = jnp.zeros_like(l_ref)
        acc_ref[...] = jnp.zeros_like(acc_ref)

    rgrp = lax.broadcasted_iota(jnp.int32, (rows, D_MODEL), 0) // (CONV_WIDTH + 1)
    cgrp = lax.broadcasted_iota(jnp.int32, (rows, D_MODEL), 1) // HEAD_DIM
    q = q_ref[...]
    qbd = jnp.where(rgrp == cgrp, q, jnp.zeros_like(q))

    s = jnp.concatenate(
        [jnp.dot(qbd, k_refs[i][...].astype(BF16), preferred_element_type=F32) for i in range(pages)],
        axis=1)
    m_prev = m_ref[...]
    m_new = jnp.maximum(m_prev, jnp.max(s, axis=-1, keepdims=True))
    alpha = jnp.exp(m_prev - m_new)
    p = jnp.exp(s - m_new)
    l_ref[...] = alpha * l_ref[...] + jnp.sum(p, axis=-1, keepdims=True)
    p = p.astype(BF16)
    pv = []
    for hd in range(N_HEADS):
        acc = None
        for i in range(pages):
            vh = v_refs[i][pl.ds(hd, page, stride=N_HEADS), :].astype(BF16)
            d = jnp.dot(p[rph * hd:rph * (hd + 1), i * page:(i + 1) * page], vh,
                        preferred_element_type=F32)
            acc = d if acc is None else acc + d
        pv.append(acc)
    acc_ref[...] = alpha * acc_ref[...] + jnp.concatenate(pv, axis=0)
    m_ref[...] = m_new

    @pl.when(j == pl.num_programs(1) - 1)
    def _():
        qf = qbd.astype(F32)
        t = lax.broadcasted_iota(jnp.int32, (rows, 1), 0) & CONV_WIDTH
        n_new = kn_ref.shape[0]
        sn = []
        for tk in range(n_new):
            st = jnp.sum(qf * kn_ref[tk:tk + 1, :], axis=-1, keepdims=True)
            sn.append(jnp.where(t >= tk, st, -jnp.inf))
        m_prev = m_ref[...]
        m_new = m_prev
        for st in sn:
            m_new = jnp.maximum(m_new, st)
        alpha = jnp.exp(m_prev - m_new)
        l = alpha * l_ref[...]
        acc = alpha * acc_ref[...]
        for tk in range(n_new):
            pn = jnp.exp(sn[tk] - m_new)
            l = l + pn
            vrow = vn_ref[tk:tk + 1, :]
            vexp = jnp.concatenate(
                [jnp.broadcast_to(vrow[:, V_DIM * hd:V_DIM * (hd + 1)], (rph, V_DIM)) for hd in range(N_HEADS)],
                axis=0)
            acc = acc + pn * vexp
        on = acc / l
        lam = _lambda(lam_ref, lam_init)
        for hd in range(N_HEADS):
            a = on[rph * hd:rph * (hd + 1), :]
            o = a - lam * pltpu.roll(a, CONV_WIDTH + 1, 0)
            o_ref[:, V_DIM * hd:V_DIM * (hd + 1)] = _rms(o, sw_ref[...]) * (1.0 - lam_init)


def _attn_sample(page_table, lam_vecs, qrep, k_new, v_new, sw, kt_pages, v_pages, lam_init):
    n_seq, n_pages = page_table.shape
    pages = min(SAMPLE_PAGES_PER_STEP, n_pages)
    rows = qrep.shape[1]

    def page_spec(a, i):
        return pl.BlockSpec((None,) + a.shape[1:], lambda s, j, pt: (pt[s, j * pages + i], 0, 0))

    per_seq = lambda a: pl.BlockSpec((None,) + a.shape[1:], lambda s, j, pt: (s, 0, 0))
    grid_spec = pltpu.PrefetchScalarGridSpec(
        num_scalar_prefetch=1,
        grid=(n_seq, n_pages // pages),
        in_specs=[pl.BlockSpec(lam_vecs.shape, lambda s, j, pt: (0, 0)),
                  per_seq(qrep), per_seq(k_new), per_seq(v_new),
                  pl.BlockSpec(sw.shape, lambda s, j, pt: (0, 0))]
        + [page_spec(kt_pages, i) for i in range(pages)] + [page_spec(v_pages, i) for i in range(pages)],
        out_specs=pl.BlockSpec((None, V7X_SUBLANES, D_MODEL), lambda s, j, pt: (s, 0, 0)),
        scratch_shapes=[pltpu.VMEM((rows, 1), F32), pltpu.VMEM((rows, 1), F32),
                        pltpu.VMEM((rows, V_DIM), F32)],
    )
    return pl.pallas_call(
        functools.partial(_attn_sample_kernel, pages=pages, lam_init=lam_init),
        grid_spec=grid_spec,
        out_shape=jax.ShapeDtypeStruct((n_seq, V7X_SUBLANES, D_MODEL), F32),
        compiler_params=pltpu.CompilerParams(dimension_semantics=("parallel", "arbitrary"),
                                             vmem_limit_bytes=VMEM_LIMIT),
        name="attn_sample",
    )(page_table, lam_vecs, qrep, k_new, v_new, sw, *([kt_pages] * pages), *([v_pages] * pages))


def _tile(n, cap):
    t = cap
    while n % t:
        t //= 2
    return t


def kernel(x_prompt, x_sample, c_prompt, c_sample, cache_k, cache_v, state_conv, page_table, w_ada, b_ada, norm_pre, norm_post, ffn1_w_gu, ffn1_w_down, w_in, conv_w, conv_b, lambda_q1, lambda_k1, lambda_q2, lambda_k2, subln_w, w_br_conv, w_br_attn, w_out, ffn2_w_gu, ffn2_w_down):
    depth = w_ada.shape[0]
    assert depth == 1, "single-layer step"
    batch, seq, _ = x_prompt.shape
    dec_batch, dec_seq, _ = x_sample.shape
    assert dec_seq == CONV_WIDTH + 1
    l = 0
    lam_init = 0.8 - 0.6 * math.exp(-0.3 * l)

    n_c = batch + dec_batch
    c_all = jnp.concatenate([c_prompt, c_sample], axis=0)
    c_all = jnp.pad(c_all, ((0, -n_c % V7X_SUBLANES), (0, 0)))
    mod_all = _ada(c_all, w_ada[l], b_ada[l].reshape(1, -1))
    mod_p = mod_all[:batch].reshape(batch, 1, -1)
    mod_s = jnp.repeat(mod_all[batch:n_c], dec_seq, axis=0)

    bf = lambda w: w[l].astype(BF16)
    wgu1, wd1, win, wbrc, wbra, wout, wgu2, wd2 = map(
        bf, (ffn1_w_gu, ffn1_w_down, w_in, w_br_conv, w_br_attn, w_out, ffn2_w_gu, ffn2_w_down))
    npre, npost = norm_pre[l], norm_post[l]
    cw, cb = conv_w[l], conv_b[l].reshape(1, -1)
    sw = subln_w[l].reshape(1, -1)
    lam_vecs = jnp.stack([lambda_q1[l], lambda_k1[l], lambda_q2[l], lambda_k2[l]])

    n_p = batch * seq
    tm_f = _tile(seq, 512)
    tm_m = _tile(seq, 256)
    xp = x_prompt.reshape(n_p, D_MODEL)
    x1p = _ffn1(xp, mod_p, npre, npost, wgu1, wd1, tm_f, seq // tm_f)
    qp, kbp, vtp, k32p, v32p, mcp, sgp, utp = _mixin(
        x1p, mod_p, npre, win, cw, cb, wbrc, None, tm_m, seq)
    tq = _tile(seq, 512)
    yap = _attn_prompt(lam_vecs, qp, kbp, vtp, sw.reshape(-1, 1), tq, lam_init)
    yp = _mixout(x1p, mcp, sgp, yap.reshape(n_p, D_MODEL), mod_p, npre, npost,
                 wbra, wout, wgu2, wd2, tm_f, seq // tm_f)

    n_s = dec_batch * dec_seq
    tm_s = _tile(n_s, 128)
    xs = x_sample.reshape(n_s, D_MODEL)
    prev_rows = (jnp.repeat(state_conv[l, :, 0], dec_seq, axis=0),
                 jnp.repeat(state_conv[l, :, 1], dec_seq, axis=0))
    x1s = _ffn1(xs, mod_s, npre, npost, wgu1, wd1, tm_s, 1)
    qs, k32s, v32s, mcs, sgs, uts = _mixin(
        x1s, mod_s, npre, win, cw, cb, wbrc, prev_rows, tm_s, None)
    qrep = jnp.tile(qs.reshape(dec_batch, dec_seq, D_MODEL), (1, 2 * N_HEADS, 1))
    n_pool, page = cache_k.shape[1], cache_k.shape[2]
    kt_pages = jnp.transpose(cache_k[l].reshape(n_pool, page, D_MODEL), (0, 2, 1))
    v_pages = cache_v[l].reshape(n_pool, page * N_HEADS, V_DIM)
    yas = _attn_sample(page_table, lam_vecs, qrep,
                       k32s.reshape(dec_batch, dec_seq, D_MODEL), v32s.reshape(dec_batch, dec_seq, D_MODEL),
                       sw, kt_pages, v_pages, lam_init)
    yas = yas[:, :dec_seq].reshape(n_s, D_MODEL)
    ys = _mixout(x1s, mcs, sgs, yas, mod_s, npre, npost, wbra, wout, wgu2, wd2, tm_s, 1)

    tiles = seq // tm_m
    conv_prompt = utp.reshape(batch, tiles, V7X_SUBLANES, D_CONV)[:, -1, -(CONV_WIDTH - 1):]
    conv_sample = uts.reshape(dec_batch, dec_seq, D_CONV)[:, -(CONV_WIDTH - 1):]
    return (yp.reshape(batch, seq, D_MODEL),
            ys.reshape(dec_batch, dec_seq, D_MODEL),
            k32p.reshape(1, batch, seq, N_HEADS, 2, HEAD_DIM),
            v32p.reshape(1, batch, seq, N_HEADS, V_DIM),
            conv_prompt[None],
            k32s.reshape(1, dec_batch, dec_seq, N_HEADS, 2, HEAD_DIM),
            v32s.reshape(1, dec_batch, dec_seq, N_HEADS, V_DIM),
            conv_sample[None])
```

```python
import functools
import math

import jax
import jax.numpy as jnp
from jax import lax
from jax.experimental import pallas as pl
from jax.experimental.pallas import tpu as pltpu

F32 = jnp.float32
BF16 = jnp.bfloat16

D_MODEL = 1024
N_HEADS = 8
HEAD_DIM = 64
V_DIM = 2 * HEAD_DIM
D_CONV = 512
CONV_WIDTH = 3
D_FF = 2816
N_ADA = 9
MACARON_W = 0.5
RMS_EPS = 1e-6
QK_SCALE = HEAD_DIM ** -0.5
LOG2_E = math.log2(math.e)

V7X_LANES = 128
V7X_SUBLANES = 8
V7X_VMEM_BYTES = 64 * 1024 * 1024
VMEM_LIMIT = V7X_VMEM_BYTES - 8 * 1024 * 1024

FF_CHUNK = D_FF // 2
ADA_COLS = N_ADA * D_MODEL // 8
ROWS_PER_HEAD = 2 * (CONV_WIDTH + 1)
SAMPLE_PAGES_PER_STEP = 16


def _rms(x, g, axis=-1):
    ms = jnp.mean(x * x, axis=axis, keepdims=True)
    return (x * lax.rsqrt(ms + RMS_EPS)) * g


def _mod_slice(mod, j):
    return mod[:, j * D_MODEL:(j + 1) * D_MODEL]


def _modulate(x, g, mod, i):
    return _rms(x, g) * (1.0 + _mod_slice(mod, 3 * i + 1)) + _mod_slice(mod, 3 * i)


def _residual(x, out, g, mod, i, res_w):
    return x + (res_w * _mod_slice(mod, 3 * i + 2)) * _rms(out, g)


def _swiglu(h, wgu_ref, wd_ref):
    acc = None
    for c in range(D_FF // FF_CHUNK):
        lo = c * FF_CHUNK
        g = jnp.dot(h, wgu_ref[:, lo:lo + FF_CHUNK], preferred_element_type=F32)
        u = jnp.dot(h, wgu_ref[:, D_FF + lo:D_FF + lo + FF_CHUNK], preferred_element_type=F32)
        a = (g * jax.nn.sigmoid(g)) * u
        d = jnp.dot(a.astype(BF16), wd_ref[lo:lo + FF_CHUNK, :], preferred_element_type=F32)
        acc = d if acc is None else acc + d
    return acc


def _ada_kernel(c_ref, w_ref, b_ref, o_ref):
    o_ref[...] = jnp.dot(c_ref[...], w_ref[...], preferred_element_type=F32,
                         precision=lax.Precision.HIGHEST) + b_ref[...]


def _ada(c, w, b):
    rows = c.shape[0]
    n = w.shape[1]
    return pl.pallas_call(
        _ada_kernel,
        grid=(n // ADA_COLS,),
        in_specs=[pl.BlockSpec((rows, D_MODEL), lambda j: (0, 0)),
                  pl.BlockSpec((D_MODEL, ADA_COLS), lambda j: (0, j)),
                  pl.BlockSpec((1, ADA_COLS), lambda j: (0, j))],
        out_specs=pl.BlockSpec((rows, ADA_COLS), lambda j: (0, j)),
        out_shape=jax.ShapeDtypeStruct((rows, n), F32),
        compiler_params=pltpu.CompilerParams(dimension_semantics=("parallel",)),
        name="ada",
    )(c, w, b)


def _resident(shape):
    return pl.BlockSpec(shape, lambda *_: (0,) * len(shape), pipeline_mode=pl.Buffered(1))


def _rows(tm, width):
    return pl.BlockSpec((tm, width), lambda i: (i, 0))


def _mod_spec(mod, tm, tiles_per_seq):
    if mod.ndim == 3:
        return pl.BlockSpec((None, 1, mod.shape[-1]), lambda i: (i // tiles_per_seq, 0, 0))
    return pl.BlockSpec((tm, mod.shape[-1]), lambda i: (i, 0))


def _ffn1_kernel(x_ref, mod_ref, npre_ref, npost_ref, wgu_ref, wd_ref, o_ref):
    x = x_ref[...]
    mod = mod_ref[...]
    h = _modulate(x, npre_ref[0:1, :], mod, 0).astype(BF16)
    f = _swiglu(h, wgu_ref, wd_ref)
    o_ref[...] = _residual(x, f, npost_ref[0:1, :], mod, 0, MACARON_W)


def _ffn1(x, mod, npre, npost, wgu, wd, tm, tiles_per_seq):
    n = x.shape[0]
    return pl.pallas_call(
        _ffn1_kernel,
        grid=(n // tm,),
        in_specs=[_rows(tm, D_MODEL), _mod_spec(mod, tm, tiles_per_seq),
                  _resident(npre.shape), _resident(npost.shape),
                  _resident(wgu.shape), _resident(wd.shape)],
        out_specs=_rows(tm, D_MODEL),
        out_shape=jax.ShapeDtypeStruct((n, D_MODEL), F32),
        compiler_params=pltpu.CompilerParams(dimension_semantics=("parallel",),
                                             vmem_limit_bytes=VMEM_LIMIT),
        name="ffn1",
    )(x, mod, npre, npost, wgu, wd)


def _mixin_kernel(*refs, tm, tiles_per_seq, sample):
    if sample:
        (x1_ref, mod_ref, npre_ref, win_ref, cw_ref, cb_ref, wbrc_ref, pa_ref, pb_ref,
         q_ref, k32_ref, v32_ref, mc_ref, sg_ref, ut_ref) = refs
    else:
        (x1_ref, mod_ref, npre_ref, win_ref, cw_ref, cb_ref, wbrc_ref,
         q_ref, kb_ref, vt_ref, k32_ref, v32_ref, mc_ref, sg_ref, ut_ref, carry_ref) = refs

    h = _modulate(x1_ref[...], npre_ref[1:2, :], mod_ref[...], 1).astype(BF16)

    def proj(lo, width):
        return jnp.dot(h, win_ref[:, lo:lo + width], preferred_element_type=F32)

    zc = proj(0, 3 * D_CONV)
    xc = zc[:, 0:D_CONV]
    bg = zc[:, D_CONV:2 * D_CONV]
    cg = zc[:, 2 * D_CONV:3 * D_CONV]
    u = cg * xc
    row = lax.broadcasted_iota(jnp.int32, (tm, D_CONV), 0)
    r1 = pltpu.roll(u, 1, 0)
    r2 = pltpu.roll(u, 2, 0)
    if sample:
        t = row & CONV_WIDTH
        um1 = jnp.where(t == 0, pb_ref[...], r1)
        um2 = jnp.where(t == 0, pa_ref[...], jnp.where(t == 1, pb_ref[...], r2))
    else:
        @pl.when(pl.program_id(0) % tiles_per_seq == 0)
        def _():
            carry_ref[...] = jnp.zeros_like(carry_ref)
        c6 = carry_ref[6:7, :]
        c7 = carry_ref[7:8, :]
        um1 = jnp.where(row == 0, c7, r1)
        um2 = jnp.where(row == 0, c6, jnp.where(row == 1, c7, r2))
        carry_ref[...] = u[tm - V7X_SUBLANES:tm, :]
    conv = cb_ref[...] + cw_ref[0:1, :] * um2
    conv = conv + cw_ref[1:2, :] * um1
    conv = conv + cw_ref[2:3, :] * u
    tail = ut_ref.shape[0]
    ut_ref[...] = u[tm - tail:tm, :]
    yc = (bg * conv).astype(BF16)

    q = (proj(3 * D_CONV, D_MODEL) * (QK_SCALE if sample else QK_SCALE * LOG2_E)).astype(BF16)
    k = proj(3 * D_CONV + D_MODEL, D_MODEL)
    v = proj(3 * D_CONV + 2 * D_MODEL, D_MODEL)
    v32_ref[...] = v
    if sample:
        q_ref[...] = q
        k32_ref[...] = k
    else:
        kb = k.astype(BF16)
        for hd in range(N_HEADS):
            cols = slice(hd * V_DIM, (hd + 1) * V_DIM)
            q_ref[hd] = q[:, cols]
            kb_ref[hd] = kb[:, cols]
            vt_ref[hd] = v[:, cols].T.astype(BF16)
            k32_ref[cols, :] = k[:, cols].T
    gc = proj(3 * D_CONV + 3 * D_MODEL, D_MODEL)
    mc_ref[...] = jax.nn.sigmoid(gc) * jnp.dot(yc, wbrc_ref[...], preferred_element_type=F32)
    ga = proj(3 * D_CONV + 4 * D_MODEL, D_MODEL)
    sg_ref[...] = jax.nn.sigmoid(ga)


def _mixin(x1, mod, npre, win, cw, cb, wbrc, prev_rows, tm, seq):
    n = x1.shape[0]
    sample = prev_rows is not None
    tiles_per_seq = 1 if sample else seq // tm
    in_specs = [_rows(tm, D_MODEL), _mod_spec(mod, tm, tiles_per_seq), _resident(npre.shape),
                _resident(win.shape), _resident(cw.shape), _resident(cb.shape), _resident(wbrc.shape)]
    args = [x1, mod, npre, win, cw, cb, wbrc]
    wide = lambda dt: jax.ShapeDtypeStruct((n, D_MODEL), dt)
    if sample:
        in_specs += [_rows(tm, D_CONV), _rows(tm, D_CONV)]
        args += list(prev_rows)
        scratch = []
        tail = tm
        head_specs = [_rows(tm, D_MODEL), _rows(tm, D_MODEL)]
        head_shapes = [wide(BF16), wide(F32)]
    else:
        scratch = [pltpu.VMEM((V7X_SUBLANES, D_CONV), F32)]
        tail = V7X_SUBLANES
        b = n // seq
        hm_spec = pl.BlockSpec((None, N_HEADS, tm, V_DIM),
                               lambda i: (i // tiles_per_seq, 0, i % tiles_per_seq, 0))
        vt_spec = pl.BlockSpec((None, N_HEADS, None, V_DIM, tm),
                               lambda i: (i // tiles_per_seq, 0, i % tiles_per_seq, 0, 0))
        kt_spec = pl.BlockSpec((None, D_MODEL, tm), lambda i: (i // tiles_per_seq, 0, i % tiles_per_seq))
        head_specs = [hm_spec, hm_spec, vt_spec, kt_spec]
        head_shapes = [jax.ShapeDtypeStruct((b, N_HEADS, seq, V_DIM), BF16)] * 2 + [
            jax.ShapeDtypeStruct((b, N_HEADS, tiles_per_seq, V_DIM, tm), BF16),
            jax.ShapeDtypeStruct((b, D_MODEL, seq), F32)]
    return pl.pallas_call(
        functools.partial(_mixin_kernel, tm=tm, tiles_per_seq=tiles_per_seq, sample=sample),
        grid=(n // tm,),
        in_specs=in_specs,
        out_specs=head_specs + [_rows(tm, D_MODEL)] * 3
        + [pl.BlockSpec((None, tail, D_CONV), lambda i: (i, 0, 0))],
        out_shape=head_shapes + [wide(F32)] * 3 + [jax.ShapeDtypeStruct((n // tm, tail, D_CONV), F32)],
        scratch_shapes=scratch,
        compiler_params=pltpu.CompilerParams(dimension_semantics=("arbitrary",),
                                             vmem_limit_bytes=VMEM_LIMIT),
        name="mixin",
    )(*args)


def _mixout_kernel(x1_ref, mc_ref, sg_ref, ya_ref, mod_ref, npre_ref, npost_ref,
                   wbra_ref, wout_ref, wgu_ref, wd_ref, o_ref):
    mod = mod_ref[...]
    att = jnp.dot(ya_ref[...].astype(BF16), wbra_ref[...], preferred_element_type=F32)
    merged = mc_ref[...] + sg_ref[...] * att
    mix = jnp.dot(merged.astype(BF16), wout_ref[...], preferred_element_type=F32)
    x2 = _residual(x1_ref[...], mix, npost_ref[1:2, :], mod, 1, 1.0)
    h = _modulate(x2, npre_ref[2:3, :], mod, 2).astype(BF16)
    f = _swiglu(h, wgu_ref, wd_ref)
    o_ref[...] = _residual(x2, f, npost_ref[2:3, :], mod, 2, MACARON_W)


def _mixout(x1, mc, sg, ya, mod, npre, npost, wbra, wout, wgu, wd, tm, tiles_per_seq):
    n = x1.shape[0]
    return pl.pallas_call(
        _mixout_kernel,
        grid=(n // tm,),
        in_specs=[_rows(tm, D_MODEL)] * 4 + [_mod_spec(mod, tm, tiles_per_seq)]
        + [_resident(a.shape) for a in (npre, npost, wbra, wout, wgu, wd)],
        out_specs=_rows(tm, D_MODEL),
        out_shape=jax.ShapeDtypeStruct((n, D_MODEL), F32),
        compiler_params=pltpu.CompilerParams(dimension_semantics=("parallel",),
                                             vmem_limit_bytes=VMEM_LIMIT),
        name="mixout",
    )(x1, mc, sg, ya, mod, npre, npost, wbra, wout, wgu, wd)


def _lambda(lam_ref, lam_init):
    lv = lam_ref[...]
    s1 = jnp.sum(lv[0:1, :] * lv[1:2, :], axis=-1, keepdims=True)
    s2 = jnp.sum(lv[2:3, :] * lv[3:4, :], axis=-1, keepdims=True)
    return jnp.exp(s1) - jnp.exp(s2) + lam_init


ATTN_COL_CHUNK = 256


def _attn_prompt_kernel(lam_ref, q_ref, k_ref, vt_ref, sw_ref, o_ref, qs_ref, st_ref, m_ref, l_ref, acc_ref,
                        *, tq, tk, lam_init):
    qi = pl.program_id(2)
    vblk = vt_ref.shape[2]
    nsub = tk // vblk
    cw = ATTN_COL_CHUNK
    nchunk = 2 * tq // cw
    q = q_ref[...]
    lane = lax.broadcasted_iota(jnp.int32, (tq, V_DIM), 1)
    zero = jnp.zeros_like(q)
    qs_ref[0:tq, :] = jnp.where(lane < HEAD_DIM, q, zero)
    qs_ref[tq:2 * tq, :] = jnp.where(lane >= HEAD_DIM, q, zero)
    m_ref[...] = jnp.full_like(m_ref, -jnp.inf)
    l_ref[...] = jnp.zeros_like(l_ref)
    acc_ref[...] = jnp.zeros_like(acc_ref)

    def chunk_live(c, d):
        return d is None or (c * cw) % tq + cw - 1 >= d * tk

    def scores(kj, slot, d=None):
        kb = k_ref[pl.ds(pl.multiple_of(kj * tk, tk), tk), :]
        for c in range(nchunk):
            if chunk_live(c, d):
                st_ref[slot, :, c * cw:(c + 1) * cw] = lax.dot_general(
                    kb, qs_ref[c * cw:(c + 1) * cw, :], (((1,), (1,)), ((), ())), preferred_element_type=F32)

    def consume(kj, slot, d=None):
        for c in range(nchunk):
            if not chunk_live(c, d):
                continue
            cols = slice(c * cw, (c + 1) * cw)
            q0 = (c * cw) % tq
            masked = d is not None and q0 < d * tk + tk - 1

            def load(r0, r1):
                st = st_ref[slot, r0:r1, cols]
                if masked:
                    key = lax.broadcasted_iota(jnp.int32, (r1 - r0, cw), 0) + (d * tk + r0)
                    qry = lax.broadcasted_iota(jnp.int32, (r1 - r0, cw), 1) + q0
                    st = jnp.where(key <= qry, st, -jnp.inf)
                return st

            m_prev = m_ref[:, cols]
            m_new = jnp.maximum(m_prev, jnp.max(load(0, tk), axis=0, keepdims=True))
            alpha = jnp.exp2(m_prev - m_new)
            lsum = alpha * l_ref[:, cols]
            pv = None
            for sub in range(nsub):
                pt = jnp.exp2(load(sub * vblk, (sub + 1) * vblk) - m_new)
                lsum = lsum + jnp.sum(pt, axis=0, keepdims=True)
                dd = jnp.dot(vt_ref[kj * nsub + sub], pt.astype(BF16), preferred_element_type=F32)
                pv = dd if pv is None else pv + dd
            l_ref[:, cols] = lsum
            acc_ref[:, cols] = alpha * acc_ref[:, cols] + pv
            m_ref[:, cols] = m_new

    scores(0, 0)

    def body(i, carry):
        kj = 2 * i
        scores(kj + 1, 1)
        consume(kj, 0)
        scores(kj + 2, 0)
        consume(kj + 1, 1)
        return carry

    nfull = qi * (tq // tk)
    lax.fori_loop(0, nfull // 2, body, 0)
    scores(nfull + 1, 1, 1)
    consume(nfull, 0, 0)
    consume(nfull + 1, 1, 1)

    lam = _lambda(lam_ref, lam_init)
    on = acc_ref[...] / l_ref[...]
    ot = on[:, 0:tq] - lam * on[:, tq:2 * tq]
    yt = _rms(ot, sw_ref[...], axis=0) * (1.0 - lam_init)
    o_ref[...] = yt.T.astype(o_ref.dtype)


def _attn_prompt(lam_vecs, q, k, vt, sw_col, tq, lam_init):
    b, _, s, _ = q.shape
    nblk, vblk = vt.shape[2], vt.shape[4]
    tk = tq // 2
    assert tk % vblk == 0 and tk % ATTN_COL_CHUNK == 0
    return pl.pallas_call(
        functools.partial(_attn_prompt_kernel, tq=tq, tk=tk, lam_init=lam_init),
        grid=(b, N_HEADS, s // tq),
        in_specs=[pl.BlockSpec(lam_vecs.shape, lambda bi, h, i: (0, 0)),
                  pl.BlockSpec((None, None, tq, V_DIM), lambda bi, h, i: (bi, h, i, 0)),
                  pl.BlockSpec((None, None, s, V_DIM), lambda bi, h, i: (bi, h, 0, 0)),
                  pl.BlockSpec((None, None, nblk, V_DIM, vblk), lambda bi, h, i: (bi, h, 0, 0, 0)),
                  pl.BlockSpec(sw_col.shape, lambda bi, h, i: (0, 0))],
        out_specs=pl.BlockSpec((None, tq, V_DIM), lambda bi, h, i: (bi, i, h)),
        out_shape=jax.ShapeDtypeStruct((b, s, D_MODEL), BF16),
        scratch_shapes=[pltpu.VMEM((2 * tq, V_DIM), BF16), pltpu.VMEM((2, tk, 2 * tq), F32),
                        pltpu.VMEM((1, 2 * tq), F32), pltpu.VMEM((1, 2 * tq), F32),
                        pltpu.VMEM((V_DIM, 2 * tq), F32)],
        compiler_params=pltpu.CompilerParams(dimension_semantics=("parallel", "parallel", "arbitrary"),
                                             vmem_limit_bytes=VMEM_LIMIT),
        name="attn_prompt",
    )(lam_vecs, q, k, vt, sw_col)


def _attn_sample_kernel(pt_ref, lam_ref, q_ref, kn_ref, vn_ref, sw_ref, *refs, pages, lam_init):
    k_refs = refs[:pages]
    v_refs = refs[pages:2 * pages]
    o_ref, m_ref, l_ref, acc_ref = refs[2 * pages:]
    j = pl.program_id(1)
    rows = q_ref.shape[0]
    page = k_refs[0].shape[1]
    rph = ROWS_PER_HEAD

    @pl.when(j == 0)
    def _():
        m_ref[...] = jnp.full_like(m_ref, -jnp.inf)
        l_ref[...] = jnp.zeros_like(l_ref)
        acc_ref[...] = jnp.zeros_like(acc_ref)

    rgrp = lax.broadcasted_iota(jnp.int32, (rows, D_MODEL), 0) // (CONV_WIDTH + 1)
    cgrp = lax.broadcasted_iota(jnp.int32, (rows, D_MODEL), 1) // HEAD_DIM
    q = q_ref[...]
    qbd = jnp.where(rgrp == cgrp, q, jnp.zeros_like(q))

    s = jnp.concatenate(
        [jnp.dot(qbd, k_refs[i][...].astype(BF16), preferred_element_type=F32) for i in range(pages)],
        axis=1)
    m_prev = m_ref[...]
    m_new = jnp.maximum(m_prev, jnp.max(s, axis=-1, keepdims=True))
    alpha = jnp.exp(m_prev - m_new)
    p = jnp.exp(s - m_new)
    l_ref[...] = alpha * l_ref[...] + jnp.sum(p, axis=-1, keepdims=True)
    p = p.astype(BF16)
    pv = []
    for hd in range(N_HEADS):
        acc = None
        for i in range(pages):
            vh = v_refs[i][pl.ds(hd, page, stride=N_HEADS), :].astype(BF16)
            d = jnp.dot(p[rph * hd:rph * (hd + 1), i * page:(i + 1) * page], vh,
                        preferred_element_type=F32)
            acc = d if acc is None else acc + d
        pv.append(acc)
    acc_ref[...] = alpha * acc_ref[...] + jnp.concatenate(pv, axis=0)
    m_ref[...] = m_new

    @pl.when(j == pl.num_programs(1) - 1)
    def _():
        qf = qbd.astype(F32)
        t = lax.broadcasted_iota(jnp.int32, (rows, 1), 0) & CONV_WIDTH
        n_new = kn_ref.shape[0]
        sn = []
        for tk in range(n_new):
            st = jnp.sum(qf * kn_ref[tk:tk + 1, :], axis=-1, keepdims=True)
            sn.append(jnp.where(t >= tk, st, -jnp.inf))
        m_prev = m_ref[...]
        m_new = m_prev
        for st in sn:
            m_new = jnp.maximum(m_new, st)
        alpha = jnp.exp(m_prev - m_new)
        l = alpha * l_ref[...]
        acc = alpha * acc_ref[...]
        for tk in range(n_new):
            pn = jnp.exp(sn[tk] - m_new)
            l = l + pn
            vrow = vn_ref[tk:tk + 1, :]
            vexp = jnp.concatenate(
                [jnp.broadcast_to(vrow[:, V_DIM * hd:V_DIM * (hd + 1)], (rph, V_DIM)) for hd in range(N_HEADS)],
                axis=0)
            acc = acc + pn * vexp
        on = acc / l
        lam = _lambda(lam_ref, lam_init)
        for hd in range(N_HEADS):
            a = on[rph * hd:rph * (hd + 1), :]
            o = a - lam * pltpu.roll(a, CONV_WIDTH + 1, 0)
            o_ref[:, V_DIM * hd:V_DIM * (hd + 1)] = _rms(o, sw_ref[...]) * (1.0 - lam_init)


def _attn_sample(page_table, lam_vecs, qrep, k_new, v_new, sw, kt_pages, v_pages, lam_init):
    n_seq, n_pages = page_table.shape
    pages = min(SAMPLE_PAGES_PER_STEP, n_pages)
    rows = qrep.shape[1]

    def page_spec(a, i):
        return pl.BlockSpec((None,) + a.shape[1:], lambda s, j, pt: (pt[s, j * pages + i], 0, 0))

    per_seq = lambda a: pl.BlockSpec((None,) + a.shape[1:], lambda s, j, pt: (s, 0, 0))
    grid_spec = pltpu.PrefetchScalarGridSpec(
        num_scalar_prefetch=1,
        grid=(n_seq, n_pages // pages),
        in_specs=[pl.BlockSpec(lam_vecs.shape, lambda s, j, pt: (0, 0)),
                  per_seq(qrep), per_seq(k_new), per_seq(v_new),
                  pl.BlockSpec(sw.shape, lambda s, j, pt: (0, 0))]
        + [page_spec(kt_pages, i) for i in range(pages)] + [page_spec(v_pages, i) for i in range(pages)],
        out_specs=pl.BlockSpec((None, V7X_SUBLANES, D_MODEL), lambda s, j, pt: (s, 0, 0)),
        scratch_shapes=[pltpu.VMEM((rows, 1), F32), pltpu.VMEM((rows, 1), F32),
                        pltpu.VMEM((rows, V_DIM), F32)],
    )
    return pl.pallas_call(
        functools.partial(_attn_sample_kernel, pages=pages, lam_init=lam_init),
        grid_spec=grid_spec,
        out_shape=jax.ShapeDtypeStruct((n_seq, V7X_SUBLANES, D_MODEL), F32),
        compiler_params=pltpu.CompilerParams(dimension_semantics=("parallel", "arbitrary"),
                                             vmem_limit_bytes=VMEM_LIMIT),
        name="attn_sample",
    )(page_table, lam_vecs, qrep, k_new, v_new, sw, *([kt_pages] * pages), *([v_pages] * pages))


def _tile(n, cap):
    t = cap
    while n % t:
        t //= 2
    return t


def kernel(x_prompt, x_sample, c_prompt, c_sample, cache_k, cache_v, state_conv, page_table, w_ada, b_ada, norm_pre, norm_post, ffn1_w_gu, ffn1_w_down, w_in, conv_w, conv_b, lambda_q1, lambda_k1, lambda_q2, lambda_k2, subln_w, w_br_conv, w_br_attn, w_out, ffn2_w_gu, ffn2_w_down):
    depth = w_ada.shape[0]
    assert depth == 1, "single-layer step"
    batch, seq, _ = x_prompt.shape
    dec_batch, dec_seq, _ = x_sample.shape
    assert dec_seq == CONV_WIDTH + 1
    l = 0
    lam_init = 0.8 - 0.6 * math.exp(-0.3 * l)

    n_c = batch + dec_batch
    c_all = jnp.concatenate([c_prompt, c_sample], axis=0)
    c_all = jnp.pad(c_all, ((0, -n_c % V7X_SUBLANES), (0, 0)))
    mod_all = _ada(c_all, w_ada[l], b_ada[l].reshape(1, -1))
    mod_p = mod_all[:batch].reshape(batch, 1, -1)
    mod_s = jnp.repeat(mod_all[batch:n_c], dec_seq, axis=0)

    bf = lambda w: w[l].astype(BF16)
    wgu1, wd1, win, wbrc, wbra, wout, wgu2, wd2 = map(
        bf, (ffn1_w_gu, ffn1_w_down, w_in, w_br_conv, w_br_attn, w_out, ffn2_w_gu, ffn2_w_down))
    npre, npost = norm_pre[l], norm_post[l]
    cw, cb = conv_w[l], conv_b[l].reshape(1, -1)
    sw = subln_w[l].reshape(1, -1)
    lam_vecs = jnp.stack([lambda_q1[l], lambda_k1[l], lambda_q2[l], lambda_k2[l]])

    n_p = batch * seq
    tm_f = _tile(seq, 512)
    tm_m = _tile(seq, 256)
    xp = x_prompt.reshape(n_p, D_MODEL)
    x1p = _ffn1(xp, mod_p, npre, npost, wgu1, wd1, tm_f, seq // tm_f)
    qp, kbp, vtp, k32p, v32p, mcp, sgp, utp = _mixin(
        x1p, mod_p, npre, win, cw, cb, wbrc, None, tm_m, seq)
    tq = _tile(seq, 1024)
    yap = _attn_prompt(lam_vecs, qp, kbp, vtp, sw.reshape(-1, 1), tq, lam_init)
    yp = _mixout(x1p, mcp, sgp, yap.reshape(n_p, D_MODEL), mod_p, npre, npost,
                 wbra, wout, wgu2, wd2, tm_f, seq // tm_f)

    n_s = dec_batch * dec_seq
    tm_s = _tile(n_s, 128)
    xs = x_sample.reshape(n_s, D_MODEL)
    prev_rows = (jnp.repeat(state_conv[l, :, 0], dec_seq, axis=0),
                 jnp.repeat(state_conv[l, :, 1], dec_seq, axis=0))
    x1s = _ffn1(xs, mod_s, npre, npost, wgu1, wd1, tm_s, 1)
    qs, k32s, v32s, mcs, sgs, uts = _mixin(
        x1s, mod_s, npre, win, cw, cb, wbrc, prev_rows, tm_s, None)
    qrep = jnp.tile(qs.reshape(dec_batch, dec_seq, D_MODEL), (1, 2 * N_HEADS, 1))
    n_pool, page = cache_k.shape[1], cache_k.shape[2]
    kt_pages = jnp.transpose(cache_k[l].reshape(n_pool, page, D_MODEL), (0, 2, 1))
    v_pages = cache_v[l].reshape(n_pool, page * N_HEADS, V_DIM)
    yas = _attn_sample(page_table, lam_vecs, qrep,
                       k32s.reshape(dec_batch, dec_seq, D_MODEL), v32s.reshape(dec_batch, dec_seq, D_MODEL),
                       sw, kt_pages, v_pages, lam_init)
    yas = yas[:, :dec_seq].reshape(n_s, D_MODEL)
    ys = _mixout(x1s, mcs, sgs, yas, mod_s, npre, npost, wbra, wout, wgu2, wd2, tm_s, 1)

    tiles = seq // tm_m
    conv_prompt = utp.reshape(batch, tiles, V7X_SUBLANES, D_CONV)[:, -1, -(CONV_WIDTH - 1):]
    conv_sample = uts.reshape(dec_batch, dec_seq, D_CONV)[:, -(CONV_WIDTH - 1):]
    return (yp.reshape(batch, seq, D_MODEL),
            ys.reshape(dec_batch, dec_seq, D_MODEL),
            jnp.transpose(k32p.reshape(batch, N_HEADS, 2, HEAD_DIM, seq), (0, 4, 1, 2, 3))[None],
            v32p.reshape(1, batch, seq, N_HEADS, V_DIM),
            conv_prompt[None],
            k32s.reshape(1, dec_batch, dec_seq, N_HEADS, 2, HEAD_DIM),
            v32s.reshape(1, dec_batch, dec_seq, N_HEADS, V_DIM),
            conv_sample[None])
```
